```python
import jax
import jax.numpy as jnp
from jax import lax
import numpy as np

D_MODEL = 1024
BATCH = 4
SEQ = 4096
DEPTH = 2
DEC_BATCH = 128
DEC_SEQ = 1
PAST_LEN = 2048
PAGE_SIZE = 128

HEAD_DIM = 64
D_MIX = D_MODEL
D_ATTN = D_MIX // 2
N_HEADS_A = D_ATTN // HEAD_DIM
D_GMLP = D_MIX // 4
N_GROUPS_B = D_GMLP // HEAD_DIM
D_CONV = D_MIX - D_ATTN - D_GMLP
D_IN = 3 * D_ATTN + 2 * D_GMLP + 2 * D_CONV
SPLIT_POINTS = (D_ATTN, 2 * D_ATTN, 3 * D_ATTN, 3 * D_ATTN + D_GMLP,
                3 * D_ATTN + 2 * D_GMLP, 3 * D_ATTN + 2 * D_GMLP + D_CONV)
MOBA_BLOCK = 256
MOBA_TOP_K = 3
MOBA_Q_CHUNK = 32
ROPE_DIM = HEAD_DIM // 4
ROPE_THETA = 500000.0
GMLP_CHUNK = 128
CONV_WIDTH = 31
FFN_CONV_WIDTH = 3
D_FF = 128 * ((8 * D_MODEL // 3 + 127) // 128)
EPS = 1e-6
NEG_INF = -1e30

kernel_name = 'hybrid_moba_gmlp_conformer_decoder_step'


def rms_norm(x, g):
    xf = x.astype(jnp.float32)
    y = xf * lax.rsqrt(jnp.mean(xf * xf, axis=-1, keepdims=True) + EPS)
    return (y * g.astype(jnp.float32)).astype(x.dtype)


def layer_norm(x, g, b):
    xf = x.astype(jnp.float32)
    mu = jnp.mean(xf, axis=-1, keepdims=True)
    var = jnp.mean(jnp.square(xf - mu), axis=-1, keepdims=True)
    y = (xf - mu) * lax.rsqrt(var + EPS) * g.astype(jnp.float32) + b.astype(jnp.float32)
    return y.astype(x.dtype)


def rotary(x, pos):
    half = ROPE_DIM // 2
    inv_freq = ROPE_THETA ** (-jnp.arange(half, dtype=jnp.float32) / half)
    ang = pos.astype(jnp.float32)[:, None] * inv_freq[None, :]
    cos = jnp.cos(ang)[None, :, None, :]
    sin = jnp.sin(ang)[None, :, None, :]
    xr = x[..., :ROPE_DIM].astype(jnp.float32)
    x1, x2 = xr[..., :half], xr[..., half:]
    rot = jnp.concatenate([x1 * cos - x2 * sin, x2 * cos + x1 * sin], axis=-1)
    return jnp.concatenate([rot.astype(x.dtype), x[..., ROPE_DIM:]], axis=-1)


def causal_dwconv(x, buf, w, b):
    width = w.shape[0]
    xp = jnp.concatenate([buf.astype(x.dtype), x], axis=1)
    y = lax.conv_general_dilated(xp, w[:, None, :].astype(x.dtype), window_strides=(1,), padding='VALID',
                                 dimension_numbers=('NWC', 'WIO', 'NWC'), feature_group_count=x.shape[-1])
    return y + b.astype(x.dtype), xp[:, xp.shape[1] - (width - 1):]


def moba_attention(q, k, v, q_pos):
    B, L, H, Dh = k.shape
    Tq = q.shape[1]
    n_blk = -(-L // MOBA_BLOCK)
    kpad = n_blk * MOBA_BLOCK - L
    kb = jnp.pad(k, ((0, 0), (0, kpad), (0, 0), (0, 0))).reshape(B, n_blk, MOBA_BLOCK, H, Dh).transpose(0, 3, 1, 2, 4)
    vb = jnp.pad(v, ((0, 0), (0, kpad), (0, 0), (0, 0))).reshape(B, n_blk, MOBA_BLOCK, H, Dh).transpose(0, 3, 1, 2, 4)
    k_mean = jnp.mean(kb.astype(jnp.float32), axis=3)
    top = min(MOBA_TOP_K, n_blk)
    scale = HEAD_DIM ** -0.5
    b_idx = jnp.arange(B)[:, None, None, None]
    h_idx = jnp.arange(H)[None, :, None, None]

    def attend(args):
        q_c, pos_c = args
        qc_len = q_c.shape[1]
        qh = q_c.transpose(0, 2, 1, 3)
        cur = pos_c // MOBA_BLOCK
        gate = jnp.einsum('bhqd,bhnd->bhqn', qh.astype(jnp.float32), k_mean)
        fully_past = jnp.arange(n_blk)[None, :] < cur[:, None]
        gate = jnp.where(fully_past, gate, NEG_INF)
        _, sel = lax.top_k(gate, top)
        own = jnp.broadcast_to(cur[None, None, :, None], (B, H, qc_len, 1)).astype(sel.dtype)
        idx = jnp.concatenate([sel, own], axis=-1)
        kg = kb[b_idx, h_idx, idx]
        vg = vb[b_idx, h_idx, idx]
        s = jnp.einsum('bhqd,bhqnkd->bhqnk', qh, kg).astype(jnp.float32) * scale
        sel_valid = jnp.concatenate([jnp.arange(top)[None, :] < cur[:, None],
                                     jnp.ones((qc_len, 1), dtype=bool)], axis=-1)
        key_pos = idx[..., None] * MOBA_BLOCK + jnp.arange(MOBA_BLOCK)
        mask = sel_valid[:, :, None] & (key_pos <= pos_c[:, None, None])
        s = jnp.where(mask, s, NEG_INF)
        p = jax.nn.softmax(s.reshape(B, H, qc_len, -1), axis=-1).reshape(s.shape).astype(vg.dtype)
        return jnp.einsum('bhqnk,bhqnkd->bqhd', p, vg)

    if Tq <= MOBA_Q_CHUNK:
        return attend((q, q_pos))
    n_c = -(-Tq // MOBA_Q_CHUNK)
    qpad = n_c * MOBA_Q_CHUNK - Tq
    qc = jnp.pad(q, ((0, 0), (0, qpad), (0, 0), (0, 0))).reshape(B, n_c, MOBA_Q_CHUNK, H, Dh).transpose(1, 0, 2, 3, 4)
    pc = jnp.pad(q_pos, (0, qpad)).reshape(n_c, MOBA_Q_CHUNK)
    out = lax.map(attend, (qc, pc))
    return out.transpose(1, 0, 2, 3, 4).reshape(B, n_c * MOBA_Q_CHUNK, H, Dh)[:, :Tq]


def gmlp_spatial_gate(u, v, ws, bs):
    B, T, _ = u.shape
    n_c = -(-T // GMLP_CHUNK)
    pad = n_c * GMLP_CHUNK - T
    vc = jnp.pad(v, ((0, 0), (0, pad), (0, 0))).reshape(B, n_c, GMLP_CHUNK, N_GROUPS_B, HEAD_DIM)
    causal = jnp.tril(jnp.ones((GMLP_CHUNK, GMLP_CHUNK), dtype=bool))
    w = jnp.where(causal, ws, jnp.zeros_like(ws))
    mixed = jnp.einsum('gts,bcsgd->bctgd', w, vc) + bs.T[None, None, :, :, None]
    return u * mixed.reshape(B, n_c * GMLP_CHUNK, D_GMLP)[:, :T]


def mixer_sublayer(h, pos, past_k, past_v, conv_buf, w_in, q_norm_g, k_norm_g, gmlp_ln_g, gmlp_ln_b,
                   gmlp_ws, gmlp_bs, conv_w, conv_b, conv_ln_g, conv_ln_b, w_out):
    B, T, _ = h.shape
    z = h @ w_in
    q, k, v, g_u, g_v, c_a, c_g = jnp.split(z, SPLIT_POINTS, axis=-1)
    q = rotary(rms_norm(q.reshape(B, T, N_HEADS_A, HEAD_DIM), q_norm_g), pos)
    k = rotary(rms_norm(k.reshape(B, T, N_HEADS_A, HEAD_DIM), k_norm_g), pos)
    v = v.reshape(B, T, N_HEADS_A, HEAD_DIM)
    if past_k is None:
        k_all, v_all = k, v
    else:
        k_all = jnp.concatenate([past_k.astype(k.dtype), k], axis=1)
        v_all = jnp.concatenate([past_v.astype(v.dtype), v], axis=1)
    attn = moba_attention(q, k_all, v_all, pos).reshape(B, T, D_ATTN)
    u_b = jax.nn.gelu(g_u)
    v_b = layer_norm(jax.nn.gelu(g_v), gmlp_ln_g, gmlp_ln_b)
    gm = gmlp_spatial_gate(u_b, v_b, gmlp_ws, gmlp_bs)
    glu = c_a * jax.nn.sigmoid(c_g)
    conv_y, conv_new = causal_dwconv(glu, conv_buf, conv_w, conv_b)
    conv_y = jax.nn.silu(layer_norm(conv_y, conv_ln_g, conv_ln_b))
    out = jnp.concatenate([attn, gm, conv_y], axis=-1) @ w_out
    return out, k, v, conv_new, v_b


def ffn_sublayer(h, ffn_buf, ffn_w_up, ffn_conv_w, ffn_conv_b, ffn_w_down):
    g, u = jnp.split(h @ ffn_w_up, 2, axis=-1)
    g_conv, ffn_new = causal_dwconv(g, ffn_buf, ffn_conv_w, ffn_conv_b)
    return (jax.nn.silu(g_conv) * u) @ ffn_w_down, ffn_new


def decoder_layer(x, c, pos, past_k, past_v, conv_buf, ffn_buf, ln1_g, ln2_g, w_ada, b_ada, w_in,
                  q_norm_g, k_norm_g, gmlp_ln_g, gmlp_ln_b, gmlp_ws, gmlp_bs, conv_w, conv_b,
                  conv_ln_g, conv_ln_b, w_out, ffn_w_up, ffn_conv_w, ffn_conv_b, ffn_w_down):
    mod = (jax.nn.silu(c) @ w_ada + b_ada)[:, None, :]
    shift1, scale1, gate1, shift2, scale2, gate2 = jnp.split(mod, 6, axis=-1)
    h = rms_norm(x, ln1_g) * (1 + scale1) + shift1
    mix, k_new, v_new, conv_new, v_gmlp = mixer_sublayer(
        h, pos, past_k, past_v, conv_buf, w_in, q_norm_g, k_norm_g, gmlp_ln_g, gmlp_ln_b,
        gmlp_ws, gmlp_bs, conv_w, conv_b, conv_ln_g, conv_ln_b, w_out)
    x = x + gate1 * mix
    h = rms_norm(x, ln2_g) * (1 + scale2) + shift2
    ff, ffn_new = ffn_sublayer(h, ffn_buf, ffn_w_up, ffn_conv_w, ffn_conv_b, ffn_w_down)
    x = x + gate2 * ff
    return x, k_new, v_new, conv_new, v_gmlp, ffn_new


def setup_inputs(seed: int = 0) -> dict:
    key = jax.random.key(seed)
    ks = jax.random.split(key, 32)
    n_pages = PAST_LEN // PAGE_SIZE
    n_used = DEC_BATCH * n_pages
    n_phys = n_used + max(1, n_used // 4)

    def nrm(k, shape, s):
        return jax.random.normal(k, shape, jnp.float32) * s

    page_table = jax.random.permutation(ks[6], n_phys)[:n_used].reshape(DEC_BATCH, n_pages).astype(jnp.int32)
    return {
        'x_prompt': nrm(ks[0], (BATCH, SEQ, D_MODEL), 1.0),
        'x_sample': nrm(ks[1], (DEC_BATCH, DEC_SEQ, D_MODEL), 1.0),
        'cache_k': nrm(ks[2], (DEPTH, n_phys, PAGE_SIZE, N_HEADS_A, HEAD_DIM), 1.0),
        'cache_v': nrm(ks[3], (DEPTH, n_phys, PAGE_SIZE, N_HEADS_A, HEAD_DIM), 1.0),
        'state_conv': nrm(ks[4], (DEPTH, DEC_BATCH, CONV_WIDTH - 1, D_CONV), 0.5),
        'state_ffn_conv': nrm(ks[5], (DEPTH, DEC_BATCH, FFN_CONV_WIDTH - 1, D_FF), 1.0),
        'page_table': page_table,
        'c_prompt': nrm(ks[7], (BATCH, D_MODEL), 1.0),
        'c_sample': nrm(ks[8], (DEC_BATCH, D_MODEL), 1.0),
        'ln1_g': 1.0 + nrm(ks[9], (DEPTH, D_MODEL), 0.1),
        'ln2_g': 1.0 + nrm(ks[10], (DEPTH, D_MODEL), 0.1),
        'w_ada': nrm(ks[11], (DEPTH, D_MODEL, 6 * D_MODEL), 0.5 * D_MODEL ** -0.5),
        'b_ada': nrm(ks[12], (DEPTH, 6 * D_MODEL), 0.02),
        'w_in': nrm(ks[13], (DEPTH, D_MODEL, D_IN), D_MODEL ** -0.5),
        'q_norm_g': 1.0 + nrm(ks[14], (DEPTH, HEAD_DIM), 0.1),
        'k_norm_g': 1.0 + nrm(ks[15], (DEPTH, HEAD_DIM), 0.1),
        'gmlp_ln_g': 1.0 + nrm(ks[16], (DEPTH, D_GMLP), 0.1),
        'gmlp_ln_b': nrm(ks[17], (DEPTH, D_GMLP), 0.02),
        'gmlp_ws': nrm(ks[18], (DEPTH, N_GROUPS_B, GMLP_CHUNK, GMLP_CHUNK), GMLP_CHUNK ** -0.5),
        'gmlp_bs': 1.0 + nrm(ks[19], (DEPTH, N_GROUPS_B, GMLP_CHUNK), 0.1),
        'conv_w': nrm(ks[20], (DEPTH, CONV_WIDTH, D_CONV), CONV_WIDTH ** -0.5),
        'conv_b': nrm(ks[21], (DEPTH, D_CONV), 0.02),
        'conv_ln_g': 1.0 + nrm(ks[22], (DEPTH, D_CONV), 0.1),
        'conv_ln_b': nrm(ks[23], (DEPTH, D_CONV), 0.02),
        'w_out': nrm(ks[24], (DEPTH, D_MIX, D_MODEL), D_MIX ** -0.5),
        'ffn_w_up': nrm(ks[25], (DEPTH, D_MODEL, 2 * D_FF), D_MODEL ** -0.5),
        'ffn_conv_w': nrm(ks[26], (DEPTH, FFN_CONV_WIDTH, D_FF), FFN_CONV_WIDTH ** -0.5),
        'ffn_conv_b': nrm(ks[27], (DEPTH, D_FF), 0.02),
        'ffn_w_down': nrm(ks[28], (DEPTH, D_FF, D_MODEL), D_FF ** -0.5),
    }


def reference(x_prompt, x_sample, cache_k, cache_v, state_conv, state_ffn_conv, page_table, c_prompt, c_sample,
              ln1_g, ln2_g, w_ada, b_ada, w_in, q_norm_g, k_norm_g, gmlp_ln_g, gmlp_ln_b, gmlp_ws, gmlp_bs,
              conv_w, conv_b, conv_ln_g, conv_ln_b, w_out, ffn_w_up, ffn_conv_w, ffn_conv_b, ffn_w_down):
    batch, seq = x_prompt.shape[0], x_prompt.shape[1]
    dec_seq = x_sample.shape[1]
    n_dec, n_pages = page_table.shape
    past_len = n_pages * cache_k.shape[2]
    pos_p = jnp.arange(seq, dtype=jnp.int32)
    pos_s = past_len + jnp.arange(dec_seq, dtype=jnp.int32)
    xp, xs = x_prompt, x_sample
    new_k_p, new_v_p, new_k_s, new_v_s = [], [], [], []
    new_conv_p, new_conv_s, new_gmlp_s, new_ffn_p, new_ffn_s = [], [], [], [], []
    for l in range(DEPTH):
        w = (ln1_g[l], ln2_g[l], w_ada[l], b_ada[l], w_in[l], q_norm_g[l], k_norm_g[l], gmlp_ln_g[l],
             gmlp_ln_b[l], gmlp_ws[l], gmlp_bs[l], conv_w[l], conv_b[l], conv_ln_g[l], conv_ln_b[l],
             w_out[l], ffn_w_up[l], ffn_conv_w[l], ffn_conv_b[l], ffn_w_down[l])
        zero_conv = jnp.zeros((batch, CONV_WIDTH - 1, D_CONV), x_prompt.dtype)
        zero_ffn = jnp.zeros((batch, FFN_CONV_WIDTH - 1, D_FF), x_prompt.dtype)
        xp, k_p, v_p, conv_p, _, ffn_p = decoder_layer(xp, c_prompt, pos_p, None, None, zero_conv, zero_ffn, *w)
        past_k = cache_k[l][page_table].reshape(n_dec, past_len, N_HEADS_A, HEAD_DIM)
        past_v = cache_v[l][page_table].reshape(n_dec, past_len, N_HEADS_A, HEAD_DIM)
        xs, k_s, v_s, conv_s, gv_s, ffn_s = decoder_layer(xs, c_sample, pos_s, past_k, past_v,
                                                          state_conv[l], state_ffn_conv[l], *w)
        new_k_p.append(k_p)
        new_v_p.append(v_p)
        new_k_s.append(k_s)
        new_v_s.append(v_s)
        new_conv_p.append(conv_p)
        new_conv_s.append(conv_s)
        new_gmlp_s.append(gv_s)
        new_ffn_p.append(ffn_p)
        new_ffn_s.append(ffn_s)
    return (xp, xs, jnp.stack(new_k_p), jnp.stack(new_v_p), jnp.stack(new_k_s), jnp.stack(new_v_s),
            jnp.stack(new_conv_p), jnp.stack(new_conv_s), jnp.stack(new_gmlp_s),
            jnp.stack(new_ffn_p), jnp.stack(new_ffn_s))
```

```python
import functools

import jax
import jax.numpy as jnp
from jax import lax
from jax.experimental import pallas as pl
from jax.experimental.pallas import tpu as pltpu

F32 = jnp.float32
BF16 = jnp.bfloat16

D_MODEL = 1024
HEAD_DIM = 64
N_HEADS = 8
D_ATTN = N_HEADS * HEAD_DIM
D_GMLP = 256
N_GROUPS = D_GMLP // HEAD_DIM
D_CONV = 256
D_REST = 2 * D_GMLP + 2 * D_CONV
D_FF = 2816
MOBA_BLOCK = 256
MOBA_TOP_K = 3
ROPE_HALF = 8
ROPE_THETA = 500000.0
GMLP_CHUNK = 128
CONV_WIDTH = 31
CONV_HALO = 32
FFN_HALO = 8
EPS = 1e-6
NEG_INF = -1e30
SM_SCALE = HEAD_DIM ** -0.5

LANES = 128
VMEM_LIMIT = 56 * 1024 * 1024

TM = 512
FF_CHUNK = 256


def _dot(a, b):
    return jnp.dot(a, b, preferred_element_type=F32)


def _dot_nt(a, b):
    return lax.dot_general(a, b, (((1,), (1,)), ((), ())), preferred_element_type=F32)


def _split(x):
    hi = x.astype(BF16)
    return hi, (x - hi.astype(F32)).astype(BF16)


def _rms(x, g):
    return x * lax.rsqrt(jnp.mean(x * x, axis=-1, keepdims=True) + EPS) * g


def _layer_norm(x, g, b):
    mu = jnp.mean(x, axis=-1, keepdims=True)
    var = jnp.mean(jnp.square(x - mu), axis=-1, keepdims=True)
    return (x - mu) * lax.rsqrt(var + EPS) * g + b


def _head_norm_rope_t(blk, g, cos_t, sin_t):
    ms = jnp.mean(blk * blk, axis=0, keepdims=True)
    y = blk * lax.rsqrt(ms + EPS) * g
    x1 = y[0:ROPE_HALF]
    x2 = y[ROPE_HALF:2 * ROPE_HALF]
    return jnp.concatenate([x1 * cos_t - x2 * sin_t, x2 * cos_t + x1 * sin_t, y[2 * ROPE_HALF:]], axis=0)


def _mod_kernel(c_ref, w_ref, b_ref, o_ref):
    a = jax.nn.silu(c_ref[...]).astype(BF16)
    o_ref[0] = _dot(a, w_ref[0].astype(BF16)) + b_ref[0]


def _modulation(c_all, w_ada, b_ada):
    depth, _, n_out = w_ada.shape
    nc = c_all.shape[0]
    tn = 1536
    return pl.pallas_call(
        _mod_kernel,
        out_shape=jax.ShapeDtypeStruct((depth, nc, n_out), F32),
        grid=(depth, n_out // tn),
        in_specs=[
            pl.BlockSpec((nc, D_MODEL), lambda l, j: (0, 0)),
            pl.BlockSpec((1, D_MODEL, tn), lambda l, j: (l, 0, j)),
            pl.BlockSpec((1, 1, tn), lambda l, j: (l, 0, j)),
        ],
        out_specs=pl.BlockSpec((1, nc, tn), lambda l, j: (l, 0, j)),
        compiler_params=pltpu.CompilerParams(
            dimension_semantics=("arbitrary", "arbitrary"), vmem_limit_bytes=VMEM_LIMIT),
        name="modulation",
    )(c_all, w_ada, b_ada.reshape(depth, 1, n_out))


def _p1_kernel(x_ref, mod_ref, ln1_ref, wr_ref, wqkvt_ref, g_ref, cost_ref, sint_ref, glg_ref, glb_ref, eye_ref,
               kt_ref, vt_ref, qtb_ref, kbf_ref, vtb_ref, kmean_ref, ub_ref, vb_ref, glu_ref):
    x = x_ref[0]
    tm = x.shape[0]
    shift1 = mod_ref[0:1, :]
    scale1 = mod_ref[1:2, :]
    h = _rms(x, ln1_ref[...]) * (1.0 + scale1) + shift1
    hb = h.astype(BF16)
    z = _dot(hb, wr_ref[...])
    zt = _dot_nt(wqkvt_ref[...], hb)

    cos_t = cost_ref[...]
    sin_t = sint_ref[...]
    for hd in range(N_HEADS):
        rows = slice(hd * HEAD_DIM, (hd + 1) * HEAD_DIM)
        qh = _head_norm_rope_t(zt[rows], g_ref[rows, :], cos_t, sin_t)
        qtb_ref[0, rows, :] = (qh * SM_SCALE).astype(BF16)
        krows = slice(D_ATTN + hd * HEAD_DIM, D_ATTN + (hd + 1) * HEAD_DIM)
        kt_ref[0, rows, :] = _head_norm_rope_t(zt[krows], g_ref[krows, :], cos_t, sin_t)
    vt = zt[2 * D_ATTN:3 * D_ATTN]
    vt_ref[0] = vt
    vtb_ref[0] = vt.astype(BF16)

    k_nat = _dot_nt(eye_ref[...], kt_ref[0].astype(BF16))
    kbf_ref[0] = k_nat.astype(BF16)
    for i in range(tm // MOBA_BLOCK):
        kmean_ref[i] = jnp.mean(k_nat[i * MOBA_BLOCK:(i + 1) * MOBA_BLOCK], axis=0, keepdims=True)

    ub_ref[0] = jax.nn.gelu(z[:, 0:D_GMLP])
    vb_ref[0] = _layer_norm(jax.nn.gelu(z[:, D_GMLP:2 * D_GMLP]), glg_ref[...], glb_ref[...]).astype(BF16)
    o = 2 * D_GMLP
    glu_ref[0] = z[:, o:o + D_CONV] * jax.nn.sigmoid(z[:, o + D_CONV:o + 2 * D_CONV])


def _prompt_project(x, mod, row0, ln1, w_rest, wqkv_t, g_tile, cos_t, sin_t, glg, glb, eye):
    b, t, _ = x.shape
    tm = min(TM, t)
    nb = t // MOBA_BLOCK
    const2 = lambda bi, ti: (0, 0)
    tok = lambda w: pl.BlockSpec((1, tm, w), lambda bi, ti: (bi, ti, 0))
    tr = pl.BlockSpec((1, D_ATTN, tm), lambda bi, ti: (bi, 0, ti))
    out_shapes = (
        jax.ShapeDtypeStruct((b, D_ATTN, t), F32),
        jax.ShapeDtypeStruct((b, D_ATTN, t), F32),
        jax.ShapeDtypeStruct((b, D_ATTN, t), BF16),
        jax.ShapeDtypeStruct((b, t, D_ATTN), BF16),
        jax.ShapeDtypeStruct((b, D_ATTN, t), BF16),
        jax.ShapeDtypeStruct((b, nb, 1, D_ATTN), F32),
        jax.ShapeDtypeStruct((b, t, D_GMLP), F32),
        jax.ShapeDtypeStruct((b, t, D_GMLP), BF16),
        jax.ShapeDtypeStruct((b, t, D_CONV), F32),
    )
    return pl.pallas_call(
        _p1_kernel,
        out_shape=out_shapes,
        grid=(b, t // tm),
        in_specs=[
            tok(D_MODEL),
            pl.BlockSpec((None, 6, D_MODEL), lambda bi, ti: (row0 + bi, 0, 0)),
            pl.BlockSpec((1, D_MODEL), const2),
            pl.BlockSpec((D_MODEL, D_REST), const2),
            pl.BlockSpec((3 * D_ATTN, D_MODEL), const2),
            pl.BlockSpec((2 * D_ATTN, tm), const2),
            pl.BlockSpec((ROPE_HALF, tm), lambda bi, ti: (0, ti)),
            pl.BlockSpec((ROPE_HALF, tm), lambda bi, ti: (0, ti)),
            pl.BlockSpec((1, D_GMLP), const2),
            pl.BlockSpec((1, D_GMLP), const2),
            pl.BlockSpec((tm, tm), const2),
        ],
        out_specs=(
            tr, tr, tr, tok(D_ATTN), tr,
            pl.BlockSpec((None, tm // MOBA_BLOCK, 1, D_ATTN), lambda bi, ti: (bi, ti, 0, 0)),
            tok(D_GMLP), tok(D_GMLP), tok(D_CONV),
        ),
        compiler_params=pltpu.CompilerParams(
            dimension_semantics=("arbitrary", "arbitrary"), vmem_limit_bytes=VMEM_LIMIT),
        name="prompt_project",
    )(x, mod, ln1, w_rest, wqkv_t, g_tile, cos_t, sin_t, glg, glb, eye)


def _moba_kernel(qt_ref, k_ref, vt_ref, kmean_ref, eye_ref, o_ref, bias_ref, ot_ref):
    j = pl.program_id(1)
    nb = kmean_ref.shape[0]
    bq = MOBA_BLOCK
    blk_iota = lax.broadcasted_iota(jnp.int32, (nb, bq), 0)
    past = blk_iota < j
    row_iota = lax.broadcasted_iota(jnp.int32, (LANES, bq), 0)
    key_iota = lax.broadcasted_iota(jnp.int32, (bq, bq), 0)
    qry_iota = lax.broadcasted_iota(jnp.int32, (bq, bq), 1)
    causal = key_iota <= qry_iota
    j0 = pl.multiple_of(j * bq, bq)

    for hd in range(N_HEADS):
        pair = hd // 2
        lanes = slice(pair * LANES, (pair + 1) * LANES)
        rows = slice(hd * HEAD_DIM, (hd + 1) * HEAD_DIM)
        in_head = (row_iota // HEAD_DIM) == (hd % 2)
        qtm = jnp.where(in_head, qt_ref[0, lanes, :], jnp.zeros((), BF16))

        km_hi, km_lo = _split(kmean_ref[:, lanes])
        gate = _dot(km_hi, qtm) + _dot(km_lo, qtm)
        gate = jnp.where(past, gate, NEG_INF)
        rank = jnp.zeros((nb, bq), F32)
        for m in range(nb):
            gm = gate[m:m + 1, :]
            beats = jnp.where(gm > gate, 1.0, jnp.where((gm == gate) & (blk_iota > m), 1.0, 0.0))
            rank = rank + beats
        bias_ref[...] = jnp.where(past & (rank < MOBA_TOP_K), 0.0, NEG_INF)

        s = _dot(k_ref[0, pl.ds(j0, bq), lanes], qtm)
        s = jnp.where(causal, s, NEG_INF)
        m0 = jnp.max(s, axis=0, keepdims=True)
        p = jnp.exp(s - m0)
        l0 = jnp.sum(p, axis=0, keepdims=True)
        acc0 = _dot(vt_ref[0, rows, pl.ds(j0, bq)], p.astype(BF16))

        def body(n, carry):
            m_i, l_i, acc = carry
            n0 = pl.multiple_of(n * bq, bq)
            s = _dot(k_ref[0, pl.ds(n0, bq), lanes], qtm) + bias_ref[pl.ds(n, 1), :]
            m_new = jnp.maximum(m_i, jnp.max(s, axis=0, keepdims=True))
            alpha = jnp.exp(m_i - m_new)
            p = jnp.exp(s - m_new)
            l_new = alpha * l_i + jnp.sum(p, axis=0, keepdims=True)
            acc = alpha * acc + _dot(vt_ref[0, rows, pl.ds(n0, bq)], p.astype(BF16))
            return m_new, l_new, acc

        _, l_f, acc_f = lax.fori_loop(0, j, body, (m0, l0, acc0))
        ot_ref[rows, :] = (acc_f / l_f).astype(BF16)

    o_ref[0] = _dot_nt(eye_ref[...], ot_ref[...]).astype(BF16)


def _prompt_attention(q_t, k_bf, v_t, kmean, eye):
    b, t, _ = k_bf.shape
    nb = t // MOBA_BLOCK
    return pl.pallas_call(
        _moba_kernel,
        out_shape=jax.ShapeDtypeStruct((b, t, D_ATTN), BF16),
        grid=(b, nb),
        in_specs=[
            pl.BlockSpec((1, D_ATTN, MOBA_BLOCK), lambda bi, j: (bi, 0, j)),
            pl.BlockSpec((1, t, D_ATTN), lambda bi, j: (bi, 0, 0)),
            pl.BlockSpec((1, D_ATTN, t), lambda bi, j: (bi, 0, 0)),
            pl.BlockSpec((None, nb, D_ATTN), lambda bi, j: (bi, 0, 0)),
            pl.BlockSpec((MOBA_BLOCK, MOBA_BLOCK), lambda bi, j: (0, 0)),
        ],
        out_specs=pl.BlockSpec((1, MOBA_BLOCK, D_ATTN), lambda bi, j: (bi, j, 0)),
        scratch_shapes=[
            pltpu.VMEM((nb, MOBA_BLOCK), F32),
            pltpu.VMEM((D_ATTN, MOBA_BLOCK), BF16),
        ],
        compiler_params=pltpu.CompilerParams(
            dimension_semantics=("arbitrary", "arbitrary"), vmem_limit_bytes=VMEM_LIMIT),
        name="prompt_moba",
    )(q_t, k_bf, v_t, kmean, eye)


def _gmlp_weights(ws_ref):
    r = lax.broadcasted_iota(jnp.int32, (GMLP_CHUNK, GMLP_CHUNK), 0)
    c = lax.broadcasted_iota(jnp.int32, (GMLP_CHUNK, GMLP_CHUNK), 1)
    return [jnp.where(r >= c, ws_ref[g], 0.0).astype(BF16) for g in range(N_GROUPS)]


def _p2a_kernel(x_ref, attn_ref, ub_ref, vb_ref, glu_ref, mod_ref, ws_ref, gbias_ref, cw_ref, cb_ref,
                clg_ref, clb_ref, wout_ref, ln2_ref,
                x1_ref, h2_ref, cbuf_ref, mix_ref):
    ti = pl.program_id(1)
    tm = x_ref.shape[1]

    @pl.when(ti == 0)
    def _():
        cbuf_ref[0:CONV_HALO, :] = jnp.zeros((CONV_HALO, D_CONV), F32)

    cbuf_ref[CONV_HALO:CONV_HALO + tm, :] = glu_ref[0]
    rc = 128
    for r in range(tm // rc):
        acc = jnp.zeros((rc, D_CONV), F32)
        for w in range(CONV_WIDTH):
            o = CONV_HALO - (CONV_WIDTH - 1) + w + r * rc
            acc = acc + cbuf_ref[o:o + rc, :] * cw_ref[w:w + 1, :]
        y = _layer_norm(acc + cb_ref[...], clg_ref[...], clb_ref[...])
        mix_ref[r * rc:(r + 1) * rc, D_ATTN + D_GMLP:] = jax.nn.silu(y).astype(BF16)
    cbuf_ref[0:CONV_HALO, :] = cbuf_ref[tm:tm + CONV_HALO, :]

    wg = _gmlp_weights(ws_ref)
    grp = lax.broadcasted_iota(jnp.int32, (GMLP_CHUNK, D_GMLP), 1) // HEAD_DIM
    for c in range(tm // GMLP_CHUNK):
        rows = slice(c * GMLP_CHUNK, (c + 1) * GMLP_CHUNK)
        vb = vb_ref[0, rows, :]
        mixed = _dot(wg[0], vb)
        for g in range(1, N_GROUPS):
            mixed = jnp.where(grp == g, _dot(wg[g], vb), mixed)
        mix_ref[rows, D_ATTN:D_ATTN + D_GMLP] = (ub_ref[0, rows, :] * (mixed + gbias_ref[...])).astype(BF16)

    mix_ref[:, 0:D_ATTN] = attn_ref[0]
    mix = _dot(mix_ref[...], wout_ref[...])
    gate1 = mod_ref[2:3, :]
    shift2 = mod_ref[3:4, :]
    scale2 = mod_ref[4:5, :]
    x1 = x_ref[0] + gate1 * mix
    x1_ref[0] = x1
    h2_ref[0] = (_rms(x1, ln2_ref[...]) * (1.0 + scale2) + shift2).astype(BF16)


def _prompt_mix(x, attn, ub, vb, glu, mod, row0, ws, gbias, cw, cb, clg, clb, w_out, ln2):
    b, t, _ = x.shape
    tm = min(TM, t)
    const2 = lambda bi, ti: (0, 0)
    tok = lambda w: pl.BlockSpec((1, tm, w), lambda bi, ti: (bi, ti, 0))
    return pl.pallas_call(
        _p2a_kernel,
        out_shape=(jax.ShapeDtypeStruct((b, t, D_MODEL), F32), jax.ShapeDtypeStruct((b, t, D_MODEL), BF16)),
        grid=(b, t // tm),
        in_specs=[
            tok(D_MODEL), tok(D_ATTN), tok(D_GMLP), tok(D_GMLP), tok(D_CONV),
            pl.BlockSpec((None, 6, D_MODEL), lambda bi, ti: (row0 + bi, 0, 0)),
            pl.BlockSpec((N_GROUPS, GMLP_CHUNK, GMLP_CHUNK), lambda bi, ti: (0, 0, 0)),
            pl.BlockSpec((GMLP_CHUNK, D_GMLP), const2),
            pl.BlockSpec((CONV_WIDTH, D_CONV), const2),
            pl.BlockSpec((1, D_CONV), const2),
            pl.BlockSpec((1, D_CONV), const2),
            pl.BlockSpec((1, D_CONV), const2),
            pl.BlockSpec((D_MODEL, D_MODEL), const2),
            pl.BlockSpec((1, D_MODEL), const2),
        ],
        out_specs=(tok(D_MODEL), tok(D_MODEL)),
        scratch_shapes=[
            pltpu.VMEM((CONV_HALO + tm, D_CONV), F32),
            pltpu.VMEM((tm, D_MODEL), BF16),
        ],
        compiler_params=pltpu.CompilerParams(
            dimension_semantics=("arbitrary", "arbitrary"), vmem_limit_bytes=VMEM_LIMIT),
        name="prompt_mix",
    )(x, attn, ub, vb, glu, mod, ws, gbias, cw, cb, clg, clb, w_out, ln2)


def _ffn_kernel(x1_ref, h2_ref, mod_ref, wup_ref, wdn_ref, fw_ref, fb_ref,
                x2_ref, glast_ref, gbuf_ref, carry_ref, acc_ref):
    ti = pl.program_id(1)
    tm = x1_ref.shape[1]
    n_chunks = D_FF // FF_CHUNK

    @pl.when(ti == 0)
    def _():
        carry_ref[...] = jnp.zeros(carry_ref.shape, F32)

    h2 = h2_ref[0]
    acc_ref[...] = jnp.zeros(acc_ref.shape, F32)

    def chunk(c, _):
        c0 = pl.multiple_of(c * FF_CHUNK, FF_CHUNK)
        g = _dot(h2, wup_ref[:, pl.ds(c0, FF_CHUNK)])
        u = _dot(h2, wup_ref[:, pl.ds(D_FF + c0, FF_CHUNK)])
        gbuf_ref[0:FFN_HALO, :] = carry_ref[:, pl.ds(c0, FF_CHUNK)]
        gbuf_ref[FFN_HALO:FFN_HALO + tm, :] = g
        fw = fw_ref[:, pl.ds(c0, FF_CHUNK)]
        gc = (gbuf_ref[FFN_HALO - 2:FFN_HALO - 2 + tm, :] * fw[0:1]
              + gbuf_ref[FFN_HALO - 1:FFN_HALO - 1 + tm, :] * fw[1:2]
              + g * fw[2:3] + fb_ref[:, pl.ds(c0, FF_CHUNK)])
        carry_ref[:, pl.ds(c0, FF_CHUNK)] = g[tm - FFN_HALO:tm]
        a = (jax.nn.silu(gc) * u).astype(BF16)
        acc_ref[...] += _dot(a, wdn_ref[pl.ds(c0, FF_CHUNK), :])
        return 0

    lax.fori_loop(0, n_chunks, chunk, 0)
    gate2 = mod_ref[5:6, :]
    x2_ref[0] = x1_ref[0] + gate2 * acc_ref[...]
    glast_ref[0] = carry_ref[...]


def _prompt_ffn(x1, h2, mod, row0, w_up, w_down, fw, fb):
    b, t, _ = x1.shape
    tm = min(TM, t)
    const2 = lambda bi, ti: (0, 0)
    tok = lambda w: pl.BlockSpec((1, tm, w), lambda bi, ti: (bi, ti, 0))
    return pl.pallas_call(
        _ffn_kernel,
        out_shape=(jax.ShapeDtypeStruct((b, t, D_MODEL), F32), jax.ShapeDtypeStruct((b, FFN_HALO, D_FF), F32)),
        grid=(b, t // tm),
        in_specs=[
            tok(D_MODEL), tok(D_MODEL),
            pl.BlockSpec((None, 6, D_MODEL), lambda bi, ti: (row0 + bi, 0, 0)),
            pl.BlockSpec((D_MODEL, 2 * D_FF), const2),
            pl.BlockSpec((D_FF, D_MODEL), const2),
            pl.BlockSpec((3, D_FF), const2),
            pl.BlockSpec((1, D_FF), const2),
        ],
        out_specs=(tok(D_MODEL), pl.BlockSpec((1, FFN_HALO, D_FF), lambda bi, ti: (bi, 0, 0))),
        scratch_shapes=[
            pltpu.VMEM((FFN_HALO + tm, FF_CHUNK), F32),
            pltpu.VMEM((FFN_HALO, D_FF), F32),
            pltpu.VMEM((tm, D_MODEL), F32),
        ],
        compiler_params=pltpu.CompilerParams(
            dimension_semantics=("arbitrary", "arbitrary"), vmem_limit_bytes=VMEM_LIMIT),
        name="prompt_ffn",
    )(x1, h2, mod, w_up, w_down, fw, fb)


def _s1_kernel(x_ref, mod_ref, ln1_ref, wr_ref, wqkvt_ref, g_ref, cost_ref, sint_ref, glg_ref, glb_ref,
               ws00_ref, bs0_ref, st_ref, cw_ref, cb_ref, clg_ref, clb_ref,
               qkvt_ref, gm_ref, vb_ref, glu_ref, cy_ref):
    x = x_ref[...]
    shift1 = mod_ref[:, 0:D_MODEL]
    scale1 = mod_ref[:, D_MODEL:2 * D_MODEL]
    h = _rms(x, ln1_ref[...]) * (1.0 + scale1) + shift1
    hb = h.astype(BF16)
    z = _dot(hb, wr_ref[...])
    zt = _dot_nt(wqkvt_ref[...], hb)
    cos_t = cost_ref[...]
    sin_t = sint_ref[...]
    for hd in range(N_HEADS):
        rows = slice(hd * HEAD_DIM, (hd + 1) * HEAD_DIM)
        qtr = _head_norm_rope_t(zt[rows], g_ref[rows, :], cos_t, sin_t)
        qkvt_ref[rows, :] = qtr * SM_SCALE
        krows = slice(D_ATTN + hd * HEAD_DIM, D_ATTN + (hd + 1) * HEAD_DIM)
        qkvt_ref[krows, :] = _head_norm_rope_t(zt[krows], g_ref[krows, :], cos_t, sin_t)
    qkvt_ref[2 * D_ATTN:3 * D_ATTN, :] = zt[2 * D_ATTN:3 * D_ATTN]

    ub = jax.nn.gelu(z[:, 0:D_GMLP])
    vb = _layer_norm(jax.nn.gelu(z[:, D_GMLP:2 * D_GMLP]), glg_ref[...], glb_ref[...])
    vb_ref[...] = vb
    gm_ref[...] = ub * (ws00_ref[...] * vb + bs0_ref[...])
    o = 2 * D_GMLP
    glu = z[:, o:o + D_CONV] * jax.nn.sigmoid(z[:, o + D_CONV:o + 2 * D_CONV])
    glu_ref[...] = glu
    acc = glu * cw_ref[CONV_WIDTH - 1:CONV_WIDTH, :]
    for w in range(CONV_WIDTH - 1):
        acc = acc + st_ref[w] * cw_ref[w:w + 1, :]
    cy_ref[...] = jax.nn.silu(_layer_norm(acc + cb_ref[...], clg_ref[...], clb_ref[...]))


def _sample_project(xs, mod_s, ln1, w_rest, wqkv_t, g_tile, cos_t, sin_t, glg, glb, ws00, bs0, st_t, cw, cb, clg, clb):
    n = xs.shape[0]
    sds = lambda w: jax.ShapeDtypeStruct((n, w), F32)
    return pl.pallas_call(
        _s1_kernel,
        out_shape=(jax.ShapeDtypeStruct((3 * D_ATTN, n), F32), sds(D_GMLP), sds(D_GMLP), sds(D_CONV), sds(D_CONV)),
        compiler_params=pltpu.CompilerParams(vmem_limit_bytes=VMEM_LIMIT),
        name="sample_project",
    )(xs, mod_s, ln1, w_rest, wqkv_t, g_tile, cos_t, sin_t, glg, glb, ws00, bs0, st_t, cw, cb, clg, clb)


def _col_to_row(col, eye):
    return jnp.sum(jnp.where(eye, col, 0.0), axis=0, keepdims=True)


def _s2_kernel(layer, pt_ref, qkvt_ref, ck_hbm, cv_hbm, o_ref, kbuf, vbuf, qkvb_ref, s_ref, pn_ref, sem):
    b = pl.program_id(0)
    n_seq = pl.num_programs(0)
    n_pages = pt_ref.shape[1]
    page = kbuf.shape[-1]
    ppb = MOBA_BLOCK // page
    n_rows = N_HEADS * n_pages

    def page_copies(seq, slot):
        out = []
        for p in range(n_pages):
            src = pt_ref[seq, p]
            out.append(pltpu.make_async_copy(ck_hbm.at[layer, src], kbuf.at[slot, p], sem.at[0, slot]))
            out.append(pltpu.make_async_copy(cv_hbm.at[layer, src], vbuf.at[slot, p], sem.at[1, slot]))
        return out

    @pl.when(b == 0)
    def _():
        for c in page_copies(0, 0):
            c.start()

    @pl.when(b + 1 < n_seq)
    def _():
        for c in page_copies(b + 1, (b + 1) % 2):
            c.start()

    slot = b % 2

    sel_b = lax.broadcasted_iota(jnp.int32, (n_seq, LANES), 0) == b
    onehot = jnp.where(sel_b, 1.0, 0.0).astype(BF16)
    hi, lo = _split(qkvt_ref[...])
    qkvb_ref[...] = _dot(hi, onehot) + _dot(lo, onehot)

    for c in page_copies(b, slot):
        c.wait()

    def page_scores(p, _):
        for hd in range(N_HEADS):
            t = kbuf[slot, p, hd] * qkvb_ref[hd * HEAD_DIM:(hd + 1) * HEAD_DIM, :]
            s_ref[pl.ds(hd * n_pages + p, 1), :] = jnp.sum(t, axis=0, keepdims=True)
        return 0

    lax.fori_loop(0, n_pages, page_scores, 0)
    s = s_ref[...]

    r_i = lax.broadcasted_iota(jnp.int32, (n_rows, n_rows), 0)
    c_i = lax.broadcasted_iota(jnp.int32, (n_rows, n_rows), 1)
    eye = r_i == c_i
    same_head = (r_i // n_pages) == (c_i // n_pages)
    blk_r = (r_i % n_pages) // ppb
    blk_c = (c_i % n_pages) // ppb
    same_blk = jnp.where(same_head & (blk_r == blk_c), 1.0, 0.0).astype(BF16)
    s_hi, s_lo = _split(s)
    blk_sum = _dot(same_blk, s_hi) + _dot(same_blk, s_lo)
    gate = jnp.sum(blk_sum, axis=1, keepdims=True) * (1.0 / MOBA_BLOCK)
    gate_t = _col_to_row(gate, eye)
    counted = same_head & ((c_i % ppb) == 0)
    beats = counted & ((gate_t > gate) | ((gate_t == gate) & (blk_c < blk_r)))
    rank = jnp.sum(jnp.where(beats, 1.0, 0.0), axis=1, keepdims=True)
    s = s + jnp.where(rank < MOBA_TOP_K, 0.0, NEG_INF)

    self_rows = []
    for hd in range(N_HEADS):
        rows = slice(hd * HEAD_DIM, (hd + 1) * HEAD_DIM)
        krows = slice(D_ATTN + hd * HEAD_DIM, D_ATTN + (hd + 1) * HEAD_DIM)
        t = jnp.sum(qkvb_ref[rows, :] * qkvb_ref[krows, :], axis=0, keepdims=True)
        self_rows.append(jnp.broadcast_to(t[:, 0:1], (n_pages, 1)))
    s_self = jnp.concatenate(self_rows, axis=0)

    m_row = jnp.max(s, axis=1, keepdims=True)
    m_head = jnp.max(jnp.where(same_head, _col_to_row(m_row, eye), NEG_INF), axis=1, keepdims=True)
    m = jnp.maximum(m_head, s_self)
    p = jnp.exp(s - m)
    p_self = jnp.exp(s_self - m)
    l_row = jnp.sum(p, axis=1, keepdims=True)
    l_head = jnp.sum(jnp.where(same_head, _col_to_row(l_row, eye), 0.0), axis=1, keepdims=True)
    inv_l = 1.0 / (l_head + p_self)
    pn_ref[...] = p * inv_l
    pn_self = p_self * inv_l

    lane_b = lax.broadcasted_iota(jnp.int32, (HEAD_DIM, n_seq), 1) == b

    @pl.when(b == 0)
    def _():
        o_ref[...] = jnp.zeros(o_ref.shape, F32)

    for hd in range(N_HEADS):
        def pv(pg, acc, hd=hd):
            return acc + vbuf[slot, pg, hd] * pn_ref[pl.ds(hd * n_pages + pg, 1), :]

        acc = lax.fori_loop(0, n_pages, pv, jnp.zeros((HEAD_DIM, page), F32))
        vrows = slice(2 * D_ATTN + hd * HEAD_DIM, 2 * D_ATTN + (hd + 1) * HEAD_DIM)
        o_h = (jnp.sum(acc, axis=1, keepdims=True)
               + pn_self[hd * n_pages:hd * n_pages + 1, :] * qkvb_ref[vrows, 0:1])
        rows = slice(hd * HEAD_DIM, (hd + 1) * HEAD_DIM)
        o_ref[rows, :] = jnp.where(lane_b, o_h, o_ref[rows, :])


def _sample_attention(layer, page_table, qkv_t, cache_kt, cache_vt):
    n, n_pages = page_table.shape
    page = cache_kt.shape[-1]
    n_rows = N_HEADS * n_pages
    grid_spec = pltpu.PrefetchScalarGridSpec(
        num_scalar_prefetch=1,
        grid=(n,),
        in_specs=[
            pl.BlockSpec((3 * D_ATTN, n), lambda b, pt: (0, 0)),
            pl.BlockSpec(memory_space=pl.ANY),
            pl.BlockSpec(memory_space=pl.ANY),
        ],
        out_specs=pl.BlockSpec((D_ATTN, n), lambda b, pt: (0, 0)),
        scratch_shapes=[
            pltpu.VMEM((2, n_pages, N_HEADS, HEAD_DIM, page), F32),
            pltpu.VMEM((2, n_pages, N_HEADS, HEAD_DIM, page), F32),
            pltpu.VMEM((3 * D_ATTN, LANES), F32),
            pltpu.VMEM((n_rows, page), F32),
            pltpu.VMEM((n_rows, page), F32),
            pltpu.SemaphoreType.DMA((2, 2)),
        ],
    )
    return pl.pallas_call(
        functools.partial(_s2_kernel, layer),
        out_shape=jax.ShapeDtypeStruct((D_ATTN, n), F32),
        grid_spec=grid_spec,
        compiler_params=pltpu.CompilerParams(
            dimension_semantics=("arbitrary",), vmem_limit_bytes=VMEM_LIMIT),
        name="sample_moba",
    )(page_table, qkv_t, cache_kt, cache_vt)


def _s3_kernel(x_ref, attnt_ref, gm_ref, cy_ref, mod_ref, eye_ref, wout_ref, ln2_ref, wup_ref, wdn_ref,
               fw_ref, fb_ref, st0_ref, st1_ref, x2_ref, g_ref):
    mod = lambda i: mod_ref[:, i * D_MODEL:(i + 1) * D_MODEL]
    attn = _dot_nt(eye_ref[...], attnt_ref[...].astype(BF16)).astype(BF16)
    mix_in = jnp.concatenate([attn, gm_ref[...].astype(BF16), cy_ref[...].astype(BF16)], axis=1)
    x1 = x_ref[...] + mod(2) * _dot(mix_in, wout_ref[...])
    h2 = (_rms(x1, ln2_ref[...]) * (1.0 + mod(4)) + mod(3)).astype(BF16)
    gu = _dot(h2, wup_ref[...])
    g = gu[:, 0:D_FF]
    u = gu[:, D_FF:]
    g_ref[...] = g
    gc = st0_ref[...] * fw_ref[0:1, :] + st1_ref[...] * fw_ref[1:2, :] + g * fw_ref[2:3, :] + fb_ref[...]
    a = (jax.nn.silu(gc) * u).astype(BF16)
    x2_ref[...] = x1 + mod(5) * _dot(a, wdn_ref[...])


def _sample_mix_ffn(xs, attn_t, gm, cy, mod_s, eye, w_out, ln2, w_up, w_down, fw, fb, st0, st1):
    n = xs.shape[0]
    return pl.pallas_call(
        _s3_kernel,
        out_shape=(jax.ShapeDtypeStruct((n, D_MODEL), F32), jax.ShapeDtypeStruct((n, D_FF), F32)),
        compiler_params=pltpu.CompilerParams(vmem_limit_bytes=VMEM_LIMIT),
        name="sample_mix_ffn",
    )(xs, attn_t, gm, cy, mod_s, eye, w_out, ln2, w_up, w_down, fw, fb, st0, st1)


def _rope_tables_t(pos):
    inv_freq = ROPE_THETA ** (-jnp.arange(ROPE_HALF, dtype=F32) / ROPE_HALF)
    ang = pos.astype(F32)[:, None] * inv_freq[None, :]
    return jnp.cos(ang).T, jnp.sin(ang).T


def _heads_first(a_t, lead):
    a = a_t.reshape(lead + (N_HEADS, HEAD_DIM, a_t.shape[-1]))
    nd = a.ndim
    return jnp.transpose(a, tuple(range(nd - 3)) + (nd - 1, nd - 3, nd - 2))


def kernel(x_prompt, x_sample, cache_k, cache_v, state_conv, state_ffn_conv, page_table, c_prompt, c_sample,
           ln1_g, ln2_g, w_ada, b_ada, w_in, q_norm_g, k_norm_g, gmlp_ln_g, gmlp_ln_b, gmlp_ws, gmlp_bs,
           conv_w, conv_b, conv_ln_g, conv_ln_b, w_out, ffn_w_up, ffn_conv_w, ffn_conv_b, ffn_w_down):
    batch, seq, _ = x_prompt.shape
    n_dec, dec_seq, _ = x_sample.shape
    depth = w_in.shape[0]
    n_pages = page_table.shape[1]
    page = cache_k.shape[2]
    past_len = n_pages * page
    assert dec_seq == 1 and seq % MOBA_BLOCK == 0 and seq % min(TM, seq) == 0
    assert past_len % MOBA_BLOCK == 0 and past_len % GMLP_CHUNK == 0 and MOBA_BLOCK % page == 0
    assert n_dec % 8 == 0 and n_dec <= LANES
    tm = min(TM, seq)

    n_c = n_dec + batch
    pad = (-n_c) % 8
    c_all = jnp.concatenate([c_sample, c_prompt, jnp.zeros((pad, D_MODEL), F32)], axis=0)
    mod = _modulation(c_all, w_ada, b_ada)

    eye_tm = jnp.eye(tm, dtype=BF16)
    eye_blk = jnp.eye(MOBA_BLOCK, dtype=BF16)
    eye_dec = jnp.eye(n_dec, dtype=BF16)
    cos_p, sin_p = _rope_tables_t(jnp.arange(seq, dtype=jnp.int32))
    cos_s, sin_s = _rope_tables_t(past_len + jnp.arange(1, dtype=jnp.int32))
    cos_s = jnp.broadcast_to(cos_s, (ROPE_HALF, n_dec))
    sin_s = jnp.broadcast_to(sin_s, (ROPE_HALF, n_dec))
    cache_kt = jnp.transpose(cache_k, (0, 1, 3, 4, 2))
    cache_vt = jnp.transpose(cache_v, (0, 1, 3, 4, 2))

    xp = x_prompt
    xs = x_sample.reshape(n_dec, D_MODEL)
    kt_p, vt_p, qkvt_s, conv_p, conv_s, gmlp_s, ffn_p, ffn_s = [], [], [], [], [], [], [], []
    for l in range(depth):
        row = lambda a: a[l][None, :]
        w_in_bf = w_in[l].astype(BF16)
        w_rest = w_in_bf[:, 3 * D_ATTN:]
        wqkv_t = w_in_bf[:, :3 * D_ATTN].T
        w_out_bf = w_out[l].astype(BF16)
        w_up_bf = ffn_w_up[l].astype(BF16)
        w_dn_bf = ffn_w_down[l].astype(BF16)
        mod_l = mod[l].reshape(n_c + pad, 6, D_MODEL)
        g_col = jnp.concatenate([jnp.tile(q_norm_g[l], N_HEADS), jnp.tile(k_norm_g[l], N_HEADS)])[:, None]
        gbias = jnp.repeat(gmlp_bs[l].T, HEAD_DIM, axis=1)

        k_t, v_t, q_tb, k_bf, v_tb, kmean, ub, vb, glu = _prompt_project(
            xp, mod_l, n_dec, row(ln1_g), w_rest, wqkv_t, jnp.broadcast_to(g_col, (2 * D_ATTN, tm)),
            cos_p, sin_p, row(gmlp_ln_g), row(gmlp_ln_b), eye_tm)
        attn = _prompt_attention(q_tb, k_bf, v_tb, kmean.reshape(batch, seq // MOBA_BLOCK, D_ATTN), eye_blk)
        x1, h2 = _prompt_mix(xp, attn, ub, vb, glu, mod_l, n_dec, gmlp_ws[l], gbias, conv_w[l], row(conv_b),
                             row(conv_ln_g), row(conv_ln_b), w_out_bf, row(ln2_g))
        xp, g_last = _prompt_ffn(x1, h2, mod_l, n_dec, w_up_bf, w_dn_bf, ffn_conv_w[l], row(ffn_conv_b))

        mod_s = mod[l, :n_dec]
        ws00 = jnp.repeat(gmlp_ws[l][:, 0, 0], HEAD_DIM)[None, :]
        bs0 = jnp.repeat(gmlp_bs[l][:, 0], HEAD_DIM)[None, :]
        st_t = jnp.transpose(state_conv[l], (1, 0, 2))
        qkv_t, gm_s, vb_s, glu_s, cy_s = _sample_project(
            xs, mod_s, row(ln1_g), w_rest, wqkv_t, jnp.broadcast_to(g_col, (2 * D_ATTN, n_dec)), cos_s, sin_s,
            row(gmlp_ln_g), row(gmlp_ln_b), ws00, bs0, st_t, conv_w[l], row(conv_b), row(conv_ln_g), row(conv_ln_b))
        attn_t = _sample_attention(l, page_table, qkv_t, cache_kt, cache_vt)
        xs, g_s = _sample_mix_ffn(xs, attn_t, gm_s, cy_s, mod_s, eye_dec, w_out_bf, row(ln2_g), w_up_bf, w_dn_bf,
                                  ffn_conv_w[l], row(ffn_conv_b), state_ffn_conv[l][:, 0], state_ffn_conv[l][:, 1])

        kt_p.append(k_t)
        vt_p.append(v_t)
        qkvt_s.append(qkv_t)
        conv_p.append(glu[:, seq - (CONV_WIDTH - 1):])
        conv_s.append(jnp.concatenate([state_conv[l][:, 1:], glu_s[:, None, :]], axis=1))
        gmlp_s.append(vb_s[:, None, :])
        ffn_p.append(g_last[:, FFN_HALO - 2:])
        ffn_s.append(jnp.stack([state_ffn_conv[l][:, 1], g_s], axis=1))

    qkvt_s = jnp.stack(qkvt_s)
    new_k_s = _heads_first(qkvt_s[:, D_ATTN:2 * D_ATTN], (depth,))[:, :, None]
    new_v_s = _heads_first(qkvt_s[:, 2 * D_ATTN:], (depth,))[:, :, None]
    return (xp, xs.reshape(n_dec, 1, D_MODEL),
            _heads_first(jnp.stack(kt_p), (depth, batch)), _heads_first(jnp.stack(vt_p), (depth, batch)),
            new_k_s, new_v_s, jnp.stack(conv_p), jnp.stack(conv_s), jnp.stack(gmlp_s),
            jnp.stack(ffn_p), jnp.stack(ffn_s))
```
